```python
import math
import jax
import jax.numpy as jnp
from jax import lax
import numpy as np

D_MODEL = 4096
BATCH = 4
SEQ = 2048
DEPTH = 2
DEC_BATCH = 8
DEC_SEQ = 4
PAST_LEN = 16384
PAGE_SIZE = 128

N_META = 16
NORM_EPS = 1e-6
L2_EPS = 1e-6
H_GDN = D_MODEL // 256
HD_GDN = 128
W_GDN = H_GDN * HD_GDN
GDN_CONV = 4
GDN_CHUNK = 64
H_FOX = D_MODEL // 256
HD_FOX = 128
W_FOX = H_FOX * HD_FOX
FOX_BLOCK = 128
FOX_F_BIAS_INIT = 3.0
HD_RWKV = 64
H_RWKV = D_MODEL // 128
W_RWKV = H_RWKV * HD_RWKV
LORA_W = 64
LORA_A = 64
LORA_G = 128
RWKV_GN_EPS = 64e-5
RWKV_IN = 3 * W_RWKV + LORA_W + LORA_A + LORA_G
N_BRANCH = 3
GDN_IN = 4 * W_GDN + 2 * H_GDN
FOX_IN = 3 * W_FOX + H_FOX
IN_WIDTH = GDN_IN + FOX_IN + RWKV_IN + N_BRANCH * D_MODEL
PEER_HEADS = 8
PEER_DK = 256
N_KEYS = 128
N_EXPERTS = N_KEYS * N_KEYS
PEER_TOPK = 16
PEER_BLOCK = 64

kernel_name = 'hybrid_gdn_fox_rwkv7_peer_decode_step'


def _split(x, sizes):
    outs, off = [], 0
    for s in sizes:
        outs.append(x[..., off:off + s])
        off += s
    return outs


def rms_norm(x, w):
    xf = x.astype(jnp.float32)
    y = xf * lax.rsqrt(jnp.mean(xf * xf, axis=-1, keepdims=True) + NORM_EPS)
    return (y * w.astype(jnp.float32)).astype(x.dtype)


def l2_normalize(x):
    return x * lax.rsqrt(jnp.sum(x * x, axis=-1, keepdims=True) + L2_EPS)


def causal_conv(x, buf, w):
    width = w.shape[0]
    L = x.shape[1]
    xc = jnp.concatenate([buf.astype(x.dtype), x], axis=1)
    xf = xc.astype(jnp.float32)
    wf = w.astype(jnp.float32)
    y = xf[:, 0:L] * wf[0]
    for i in range(1, width):
        y = y + xf[:, i:i + L] * wf[i]
    return y, xc[:, L:]


def gdn_chunked(q, k, v, g, beta, S0, chunk):
    B, L, H, DK = q.shape
    DV = v.shape[-1]
    pad = (-L) % chunk
    if pad:
        q, k, v, g, beta = [jnp.pad(a, [(0, 0), (pad, 0)] + [(0, 0)] * (a.ndim - 2)) for a in (q, k, v, g, beta)]
    n = (L + pad) // chunk

    def to_chunks(a):
        a = a.reshape((B, n, chunk, H) + a.shape[3:])
        return jnp.moveaxis(a, (1, 3), (0, 2))

    qc, kc, vc, gc, bc = [to_chunks(a) for a in (q, k, v, g, beta)]
    gcum = jnp.cumsum(gc, axis=-1)
    idx = jnp.arange(chunk)
    lower = idx[:, None] >= idx[None, :]
    strict = idx[:, None] > idx[None, :]
    decay = jnp.exp(jnp.where(lower, gcum[..., :, None] - gcum[..., None, :], -jnp.inf))
    kb = kc * bc[..., None]
    a_mat = jnp.where(strict, jnp.einsum('nbhid,nbhjd->nbhij', kb, kc) * decay, 0.0)
    rhs = jnp.concatenate([vc * bc[..., None], kb * jnp.exp(gcum)[..., None]], axis=-1)
    eye = jnp.eye(chunk, dtype=a_mat.dtype)
    sol = lax.linalg.triangular_solve(a_mat + eye, rhs, left_side=True, lower=True, unit_diagonal=True)
    u, w = sol[..., :DV], sol[..., DV:]
    intra = jnp.einsum('nbhid,nbhjd->nbhij', qc, kc) * decay

    def step(S, inp):
        q_i, k_i, u_i, w_i, g_i, a_i = inp
        v_new = u_i - jnp.einsum('bhck,bhkv->bhcv', w_i, S)
        o = (jnp.einsum('bhck,bhkv->bhcv', q_i * jnp.exp(g_i)[..., None], S)
             + jnp.einsum('bhij,bhjv->bhiv', a_i, v_new))
        g_last = g_i[..., -1:]
        S = (S * jnp.exp(g_last)[..., None]
             + jnp.einsum('bhck,bhcv->bhkv', k_i * jnp.exp(g_last - g_i)[..., None], v_new))
        return S, o

    S, o = lax.scan(step, S0, (qc, kc, u, w, gcum, intra))
    o = jnp.moveaxis(o, (0, 2), (1, 3)).reshape(B, n * chunk, H, DV)[:, pad:]
    return o, S


def gdn_branch(z, conv_w, A_log, dt_bias, norm_w, conv_buf, S0, chunk):
    B, L, _ = z.shape
    qkv, gate_z, a, b = _split(z, (3 * W_GDN, W_GDN, H_GDN, H_GDN))
    qkv_c, new_buf = causal_conv(qkv, conv_buf, conv_w)
    qkv_c = jax.nn.silu(qkv_c)
    q, k, v = [t.reshape(B, L, H_GDN, HD_GDN) for t in _split(qkv_c, (W_GDN, W_GDN, W_GDN))]
    q = l2_normalize(q) * (HD_GDN ** -0.5)
    k = l2_normalize(k)
    g = -jnp.exp(A_log.astype(jnp.float32)) * jax.nn.softplus(a.astype(jnp.float32) + dt_bias.astype(jnp.float32))
    beta = jax.nn.sigmoid(b.astype(jnp.float32))
    o, S = gdn_chunked(q, k, v, g, beta, S0.astype(jnp.float32), chunk)
    o = o * lax.rsqrt(jnp.mean(o * o, axis=-1, keepdims=True) + NORM_EPS) * norm_w.astype(jnp.float32)
    o = o * jax.nn.silu(gate_z.astype(jnp.float32)).reshape(B, L, H_GDN, HD_GDN)
    return o.reshape(B, L, W_GDN), new_buf, S.astype(z.dtype)


def _fox_prompt(q, k, v, logf):
    L = q.shape[1]
    scale = HD_FOX ** -0.5
    F = jnp.transpose(jnp.cumsum(logf, axis=1), (0, 2, 1))
    pos = jnp.arange(L)
    starts = [0] + list(range(N_META, L, FOX_BLOCK))
    ends = starts[1:] + [L]
    outs = []
    for s0, s1 in zip(starts, ends):
        logits = (jnp.einsum('bqhd,bkhd->bhqk', q[:, s0:s1], k[:, :s1]) * scale
                  + F[:, :, s0:s1, None] - F[:, :, None, :s1])
        logits = jnp.where(pos[s0:s1, None] >= pos[None, :s1], logits, -jnp.inf)
        p = jax.nn.softmax(logits, axis=-1)
        outs.append(jnp.einsum('bhqk,bkhd->bqhd', p, v[:, :s1]))
    return jnp.concatenate(outs, axis=1)


def _fox_sample(q, k, v, logf, k_past, v_past, logf_past):
    S = q.shape[1]
    P = k_past.shape[1]
    scale = HD_FOX ** -0.5
    kp = k_past.astype(jnp.float32)
    vp = v_past.astype(jnp.float32)
    lfp = logf_past.astype(jnp.float32)
    suffix = lax.cumsum(lfp, axis=1, reverse=True) - lfp
    cn = jnp.transpose(jnp.cumsum(logf, axis=1), (0, 2, 1))
    logit_past = (jnp.einsum('bqhd,bkhd->bhqk', q, kp) * scale
                  + cn[..., :, None] + jnp.transpose(suffix, (0, 2, 1))[..., None, :])
    pos = jnp.arange(S)
    logit_new = jnp.einsum('bqhd,bkhd->bhqk', q, k) * scale + cn[..., :, None] - cn[..., None, :]
    logit_new = jnp.where(pos[:, None] >= pos[None, :], logit_new, -jnp.inf)
    p = jax.nn.softmax(jnp.concatenate([logit_past, logit_new], axis=-1), axis=-1)
    return (jnp.einsum('bhqk,bkhd->bqhd', p[..., :P], vp)
            + jnp.einsum('bhqk,bkhd->bqhd', p[..., P:], v))


def fox_branch(z, f_bias, fox_past):
    B, L, _ = z.shape
    qr, kr, vr, f = _split(z, (W_FOX, W_FOX, W_FOX, H_FOX))
    k_rows = kr.reshape(B, L, H_FOX, HD_FOX)
    v_rows = vr.reshape(B, L, H_FOX, HD_FOX)
    q = qr.reshape(B, L, H_FOX, HD_FOX).astype(jnp.float32)
    k = k_rows.astype(jnp.float32)
    v = v_rows.astype(jnp.float32)
    logf = jax.nn.log_sigmoid(f.astype(jnp.float32) + f_bias.astype(jnp.float32))
    if fox_past is None:
        o = _fox_prompt(q, k, v, logf)
    else:
        o = _fox_sample(q, k, v, logf, fox_past[0], fox_past[1], fox_past[2])
    return o.reshape(B, L, W_FOX), k_rows, v_rows, logf.astype(z.dtype)


def rwkv_branch(z, lp, shift_buf, S0):
    B, L, _ = z.shape
    zf = z.astype(jnp.float32)
    prev = jnp.concatenate([shift_buf.astype(jnp.float32)[:, None], zf[:, :-1]], axis=1)
    xs = zf + lp['rwkv_mu'] * (prev - zf)
    new_shift = z[:, -1]
    r, k, v, wd, ad, gd = _split(xs, (W_RWKV, W_RWKV, W_RWKV, LORA_W, LORA_A, LORA_G))
    log_w = -math.exp(-0.5) * jax.nn.sigmoid(lp['rwkv_w0'] + jnp.tanh(wd) @ lp['rwkv_w2'])
    iclr = jax.nn.sigmoid(lp['rwkv_a0'] + ad @ lp['rwkv_a2'])
    gate = jax.nn.sigmoid(gd) @ lp['rwkv_g2']
    heads = lambda t: t.reshape(B, L, H_RWKV, HD_RWKV)
    kk = l2_normalize(heads(k * lp['rwkv_k_k']))
    k = k * (1.0 + (iclr - 1.0) * lp['rwkv_k_a'])
    r, w, k, v, iclr = [heads(t) for t in (r, jnp.exp(log_w), k, v, iclr)]

    def step(S, inp):
        r_t, w_t, k_t, v_t, kk_t, a_t = inp
        sa = jnp.einsum('bhvk,bhk->bhv', S, -kk_t)
        S = (S * w_t[:, :, None, :] + sa[..., None] * (kk_t * a_t)[:, :, None, :]
             + v_t[..., :, None] * k_t[:, :, None, :])
        return S, jnp.einsum('bhvk,bhk->bhv', S, r_t)

    seq_first = lambda t: jnp.moveaxis(t, 1, 0)
    S, o = lax.scan(step, S0.astype(jnp.float32), tuple(seq_first(t) for t in (r, w, k, v, kk, iclr)))
    o = jnp.moveaxis(o, 0, 1)
    mu = jnp.mean(o, axis=-1, keepdims=True)
    var = jnp.mean(jnp.square(o - mu), axis=-1, keepdims=True)
    o = ((o - mu) * lax.rsqrt(var + RWKV_GN_EPS)).reshape(B, L, W_RWKV) * lp['rwkv_ln_w'] + lp['rwkv_ln_b']
    bonus = jnp.sum(r * k * lp['rwkv_r_k'], axis=-1, keepdims=True) * v
    o = (o + bonus.reshape(B, L, W_RWKV)) * gate
    return o, new_shift, S.astype(z.dtype)


def token_mixers(h, lp, gdn_buf, gdn_S0, rwkv_buf, rwkv_S0, fox_past, gdn_chunk):
    B, L, _ = h.shape
    dt = h.dtype
    proj = jnp.einsum('bld,de->ble', h, lp['w_in'])
    z_gdn, z_fox, z_rwkv, z_gate = _split(proj, (GDN_IN, FOX_IN, RWKV_IN, N_BRANCH * D_MODEL))
    y_gdn, gdn_buf_new, gdn_S = gdn_branch(z_gdn, lp['gdn_conv_w'], lp['gdn_A_log'], lp['gdn_dt_bias'],
                                           lp['gdn_norm_w'], gdn_buf, gdn_S0, gdn_chunk)
    y_fox, fox_k, fox_v, fox_logf = fox_branch(z_fox, lp['fox_f_bias'], fox_past)
    y_rwkv, rwkv_buf_new, rwkv_S = rwkv_branch(z_rwkv, lp, rwkv_buf, rwkv_S0)
    gates = jax.nn.sigmoid(z_gate.astype(jnp.float32)).reshape(B, L, N_BRANCH, D_MODEL)
    merged = (gates[:, :, 0] * jnp.einsum('blc,cd->bld', y_gdn.astype(dt), lp['w_branch_gdn'])
              + gates[:, :, 1] * jnp.einsum('blc,cd->bld', y_fox.astype(dt), lp['w_branch_fox'])
              + gates[:, :, 2] * jnp.einsum('blc,cd->bld', y_rwkv.astype(dt), lp['w_branch_rwkv']))
    out = jnp.einsum('bld,de->ble', merged.astype(dt), lp['w_out'])
    return out, (fox_k, fox_v, fox_logf, gdn_buf_new, gdn_S, rwkv_buf_new, rwkv_S)


def peer_ffn(h, wq, subkeys, u_tab, v_tab):
    B, L, D = h.shape
    T = B * L
    x = h.reshape(T, D)
    q = jnp.einsum('td,de->te', x, wq).astype(jnp.float32).reshape(T, PEER_HEADS, PEER_DK)
    half = PEER_DK // 2
    sk = subkeys.astype(jnp.float32)
    s1 = jnp.einsum('thd,hkd->thk', q[..., :half], sk[:, 0])
    s2 = jnp.einsum('thd,hkd->thk', q[..., half:], sk[:, 1])
    v1, i1 = lax.top_k(s1, PEER_TOPK)
    v2, i2 = lax.top_k(s2, PEER_TOPK)
    cand = (v1[..., :, None] + v2[..., None, :]).reshape(T, PEER_HEADS, PEER_TOPK * PEER_TOPK)
    sc, ci = lax.top_k(cand, PEER_TOPK)
    e1 = jnp.take_along_axis(i1, ci // PEER_TOPK, axis=-1)
    e2 = jnp.take_along_axis(i2, ci % PEER_TOPK, axis=-1)
    experts = (e1 * N_KEYS + e2).reshape(T, PEER_HEADS * PEER_TOPK)
    gates = jax.nn.softmax(sc, axis=-1).reshape(T, PEER_HEADS * PEER_TOPK)
    pad = (-T) % PEER_BLOCK
    nb = (T + pad) // PEER_BLOCK
    xp = jnp.pad(x, ((0, pad), (0, 0))).reshape(nb, PEER_BLOCK, D)
    ep = jnp.pad(experts, ((0, pad), (0, 0))).reshape(nb, PEER_BLOCK, PEER_HEADS * PEER_TOPK)
    gp = jnp.pad(gates, ((0, pad), (0, 0))).reshape(nb, PEER_BLOCK, PEER_HEADS * PEER_TOPK)

    def block(args):
        xb, eb, gb = args
        hb = jnp.einsum('cd,ced->ce', xb, u_tab[eb]).astype(jnp.float32)
        ab = (gb * jax.nn.gelu(hb, approximate=False)).astype(xb.dtype)
        return jnp.einsum('ce,ced->cd', ab, v_tab[eb])

    out = lax.map(block, (xp, ep, gp))
    return out.reshape(nb * PEER_BLOCK, D)[:T].reshape(B, L, D)


def trunk_layer(x, lp, gdn_buf, gdn_S0, rwkv_buf, rwkv_S0, fox_past, gdn_chunk):
    h = rms_norm(x, lp['norm_mix'])
    mixed, states = token_mixers(h, lp, gdn_buf, gdn_S0, rwkv_buf, rwkv_S0, fox_past, gdn_chunk)
    x = x + mixed
    x = x + peer_ffn(rms_norm(x, lp['norm_ffn']), lp['peer_wq'], lp['peer_subkeys'], lp['peer_u'], lp['peer_v'])
    return x, states


def setup_inputs(seed: int = 0) -> dict:
    key = jax.random.key(seed)
    ks = iter(jax.random.split(key, 64))
    nrm = lambda shape, scale: jax.random.normal(next(ks), shape, jnp.float32) * scale
    uni = lambda shape, lo, hi: jax.random.uniform(next(ks), shape, jnp.float32, lo, hi)
    n_pages = PAST_LEN // PAGE_SIZE
    n_used = DEC_BATCH * n_pages
    n_pool = n_used + n_used // 4
    x_prompt = nrm((BATCH, SEQ, D_MODEL), 1.0)
    x_sample = nrm((DEC_BATCH, DEC_SEQ, D_MODEL), 1.0)
    cache_fox_k = nrm((DEPTH, n_pool, PAGE_SIZE, H_FOX, HD_FOX), 1.0)
    cache_fox_v = nrm((DEPTH, n_pool, PAGE_SIZE, H_FOX, HD_FOX), 1.0)
    cache_fox_logf = jax.nn.log_sigmoid(FOX_F_BIAS_INIT + nrm((DEPTH, n_pool, PAGE_SIZE, H_FOX), 1.0))
    state_gdn_conv = nrm((DEPTH, DEC_BATCH, GDN_CONV - 1, 3 * W_GDN), 1.0)
    state_gdn_S = nrm((DEPTH, DEC_BATCH, H_GDN, HD_GDN, HD_GDN), 0.1)
    state_rwkv_shift = nrm((DEPTH, DEC_BATCH, RWKV_IN), 1.0)
    state_rwkv_S = nrm((DEPTH, DEC_BATCH, H_RWKV, HD_RWKV, HD_RWKV), 0.5)
    page_table = jax.random.permutation(next(ks), n_pool)[:n_used].reshape(DEC_BATCH, n_pages).astype(jnp.int32)
    dt = jnp.exp(uni((DEPTH, H_GDN), math.log(1e-3), math.log(1e-1)))
    gdn_dt_bias = dt + jnp.log(-jnp.expm1(-dt))
    return {
        'x_prompt': x_prompt,
        'x_sample': x_sample,
        'cache_fox_k': cache_fox_k,
        'cache_fox_v': cache_fox_v,
        'cache_fox_logf': cache_fox_logf,
        'state_gdn_conv': state_gdn_conv,
        'state_gdn_S': state_gdn_S,
        'state_rwkv_shift': state_rwkv_shift,
        'state_rwkv_S': state_rwkv_S,
        'page_table': page_table,
        'meta_tokens': nrm((N_META, D_MODEL), 1.0),
        'norm_mix': 1.0 + nrm((DEPTH, D_MODEL), 0.02),
        'norm_ffn': 1.0 + nrm((DEPTH, D_MODEL), 0.02),
        'norm_final': 1.0 + nrm((D_MODEL,), 0.02),
        'w_in': nrm((DEPTH, D_MODEL, IN_WIDTH), D_MODEL ** -0.5),
        'gdn_conv_w': nrm((DEPTH, GDN_CONV, 3 * W_GDN), GDN_CONV ** -0.5),
        'gdn_A_log': jnp.log(uni((DEPTH, H_GDN), 1.0, 16.0)),
        'gdn_dt_bias': gdn_dt_bias,
        'gdn_norm_w': 1.0 + nrm((DEPTH, HD_GDN), 0.02),
        'fox_f_bias': FOX_F_BIAS_INIT + nrm((DEPTH, H_FOX), 0.1),
        'rwkv_mu': uni((DEPTH, RWKV_IN), 0.0, 1.0),
        'rwkv_w0': nrm((DEPTH, W_RWKV), 0.5),
        'rwkv_w2': nrm((DEPTH, LORA_W, W_RWKV), LORA_W ** -0.5),
        'rwkv_a0': nrm((DEPTH, W_RWKV), 0.1),
        'rwkv_a2': nrm((DEPTH, LORA_A, W_RWKV), LORA_A ** -0.5),
        'rwkv_g2': nrm((DEPTH, LORA_G, W_RWKV), LORA_G ** -0.5),
        'rwkv_k_k': 0.85 + nrm((DEPTH, W_RWKV), 0.05),
        'rwkv_k_a': 1.0 + nrm((DEPTH, W_RWKV), 0.05),
        'rwkv_r_k': nrm((DEPTH, H_RWKV, HD_RWKV), 0.1),
        'rwkv_ln_w': 1.0 + nrm((DEPTH, W_RWKV), 0.02),
        'rwkv_ln_b': nrm((DEPTH, W_RWKV), 0.02),
        'w_branch_gdn': nrm((DEPTH, W_GDN, D_MODEL), W_GDN ** -0.5),
        'w_branch_fox': nrm((DEPTH, W_FOX, D_MODEL), W_FOX ** -0.5),
        'w_branch_rwkv': nrm((DEPTH, W_RWKV, D_MODEL), W_RWKV ** -0.5),
        'w_out': nrm((DEPTH, D_MODEL, D_MODEL), D_MODEL ** -0.5),
        'peer_wq': nrm((DEPTH, D_MODEL, PEER_HEADS * PEER_DK), D_MODEL ** -0.5),
        'peer_subkeys': nrm((DEPTH, PEER_HEADS, 2, N_KEYS, PEER_DK // 2), (PEER_DK // 2) ** -0.5),
        'peer_u': nrm((DEPTH, N_EXPERTS, D_MODEL), D_MODEL ** -0.5),
        'peer_v': nrm((DEPTH, N_EXPERTS, D_MODEL), (PEER_HEADS * PEER_TOPK) ** -0.5),
    }


def reference(x_prompt, x_sample, cache_fox_k, cache_fox_v, cache_fox_logf, state_gdn_conv, state_gdn_S,
              state_rwkv_shift, state_rwkv_S, page_table, meta_tokens, norm_mix, norm_ffn, norm_final, w_in,
              gdn_conv_w, gdn_A_log, gdn_dt_bias, gdn_norm_w, fox_f_bias, rwkv_mu, rwkv_w0, rwkv_w2, rwkv_a0,
              rwkv_a2, rwkv_g2, rwkv_k_k, rwkv_k_a, rwkv_r_k, rwkv_ln_w, rwkv_ln_b, w_branch_gdn, w_branch_fox,
              w_branch_rwkv, w_out, peer_wq, peer_subkeys, peer_u, peer_v):
    dt = x_prompt.dtype
    B = x_prompt.shape[0]
    DB = x_sample.shape[0]
    xp = jnp.concatenate([jnp.broadcast_to(meta_tokens.astype(dt)[None], (B, N_META, D_MODEL)), x_prompt], axis=1)
    xs = x_sample
    prompt_states, sample_states = [], []
    for l in range(DEPTH):
        lp = {
            'norm_mix': norm_mix[l], 'norm_ffn': norm_ffn[l], 'w_in': w_in[l],
            'gdn_conv_w': gdn_conv_w[l], 'gdn_A_log': gdn_A_log[l], 'gdn_dt_bias': gdn_dt_bias[l],
            'gdn_norm_w': gdn_norm_w[l], 'fox_f_bias': fox_f_bias[l],
            'rwkv_mu': rwkv_mu[l], 'rwkv_w0': rwkv_w0[l], 'rwkv_w2': rwkv_w2[l], 'rwkv_a0': rwkv_a0[l],
            'rwkv_a2': rwkv_a2[l], 'rwkv_g2': rwkv_g2[l], 'rwkv_k_k': rwkv_k_k[l], 'rwkv_k_a': rwkv_k_a[l],
            'rwkv_r_k': rwkv_r_k[l], 'rwkv_ln_w': rwkv_ln_w[l], 'rwkv_ln_b': rwkv_ln_b[l],
            'w_branch_gdn': w_branch_gdn[l], 'w_branch_fox': w_branch_fox[l], 'w_branch_rwkv': w_branch_rwkv[l],
            'w_out': w_out[l], 'peer_wq': peer_wq[l], 'peer_subkeys': peer_subkeys[l],
            'peer_u': peer_u[l], 'peer_v': peer_v[l],
        }
        xp, st_p = trunk_layer(
            xp, lp,
            jnp.zeros((B, GDN_CONV - 1, 3 * W_GDN), dt), jnp.zeros((B, H_GDN, HD_GDN, HD_GDN), jnp.float32),
            jnp.zeros((B, RWKV_IN), dt), jnp.zeros((B, H_RWKV, HD_RWKV, HD_RWKV), jnp.float32),
            None, GDN_CHUNK)
        fox_past = (cache_fox_k[l, page_table].reshape(DB, -1, H_FOX, HD_FOX),
                    cache_fox_v[l, page_table].reshape(DB, -1, H_FOX, HD_FOX),
                    cache_fox_logf[l, page_table].reshape(DB, -1, H_FOX))
        xs, st_s = trunk_layer(xs, lp, state_gdn_conv[l], state_gdn_S[l], state_rwkv_shift[l], state_rwkv_S[l],
                               fox_past, min(GDN_CHUNK, xs.shape[1]))
        prompt_states.append(st_p)
        sample_states.append(st_s)
    y_prompt = rms_norm(xp[:, N_META:], norm_final)
    y_sample = rms_norm(xs, norm_final)
    fk_p, fv_p, fl_p, gc_p, gS_p, rs_p, rS_p = [jnp.stack([s[i] for s in prompt_states]) for i in range(7)]
    fk_s, fv_s, fl_s, gc_s, gS_s, rs_s, rS_s = [jnp.stack([s[i] for s in sample_states]) for i in range(7)]
    return (y_prompt, y_sample, fk_p, fv_p, fl_p, gc_p, gS_p, rs_p, rS_p,
            fk_s, fv_s, fl_s, gc_s, gS_s, rs_s, rS_s)
```

```python
import functools
import math

import jax
import jax.numpy as jnp
from jax import lax
from jax.experimental import pallas as pl
from jax.experimental.pallas import tpu as pltpu

F32 = jnp.float32
BF16 = jnp.bfloat16
HIGHEST = lax.Precision.HIGHEST

LANES = 128
V7X_VMEM_LIMIT = 56 * 1024 * 1024

NORM_EPS = 1e-6
L2_EPS = 1e-6
RWKV_GN_EPS = 64e-5
RWKV_HEAD = 64
RWKV_DECAY_SCALE = -math.exp(-0.5)
PEER_TOPK = 16
CHUNK = 64
ROWS = 128
NEG = -1e30


def _params(*sem):
    return pltpu.CompilerParams(dimension_semantics=sem, vmem_limit_bytes=V7X_VMEM_LIMIT)


def _dot(a, b):
    return jnp.dot(a.astype(BF16), b.astype(BF16), preferred_element_type=F32)


def _dot_nt(a, b):
    return lax.dot_general(a.astype(BF16), b.astype(BF16), (((1,), (1,)), ((), ())),
                           preferred_element_type=F32)


def _dot_hi(a, b):
    return jnp.dot(a, b, precision=HIGHEST, preferred_element_type=F32)


def _dot_nt_hi(a, b):
    return lax.dot_general(a, b, (((1,), (1,)), ((), ())), precision=HIGHEST,
                           preferred_element_type=F32)


def _iota(shape, dim):
    return lax.broadcasted_iota(jnp.int32, shape, dim)


def _eye(n):
    return (_iota((n, n), 0) == _iota((n, n), 1)).astype(F32)


def _col_bcast(row):
    n = row.shape[1]
    return _dot_nt_hi(_eye(n), jnp.broadcast_to(row, (LANES, n)))


def _transpose_hi(x):
    return _dot_nt_hi(_eye(x.shape[1]), x)


def _softplus(x):
    return jnp.maximum(x, 0.0) + jnp.log1p(jnp.exp(-jnp.abs(x)))


def _sigmoid(x):
    return jax.nn.sigmoid(x)


def _unit_lower_inverse(n_mat, steps):
    c = n_mat.shape[0]
    t = _eye(c) + n_mat
    p = n_mat
    for _ in range(steps - 1):
        p = _dot_hi(p, p)
        t = t + _dot_hi(t, p)
    return t


def _norm_body(*refs, has_ot, out_x, out_h, out_ht, out_y):
    refs = list(refs)
    x_ref = refs.pop(0)
    ot_ref = refs.pop(0) if has_ot else None
    w_ref = refs.pop(0)
    x = x_ref[...]
    if has_ot:
        x = x + ot_ref[...].T
    if out_x:
        refs.pop(0)[...] = x
    y = x * lax.rsqrt(jnp.mean(x * x, axis=-1, keepdims=True) + NORM_EPS) * w_ref[...]
    if out_h:
        refs.pop(0)[...] = y.astype(BF16)
    if out_ht:
        refs.pop(0)[...] = y.T.astype(BF16)
    if out_y:
        refs.pop(0)[...] = y


def _norm(x, w, ot=None, *, out_x=False, out_h=False, out_ht=False, out_y=False, tr=256):
    t, d = x.shape
    row = pl.BlockSpec((tr, d), lambda i: (i, 0))
    col = pl.BlockSpec((d, tr), lambda i: (0, i))
    ins, in_specs = [x], [row]
    if ot is not None:
        ins.append(ot)
        in_specs.append(col)
    ins.append(w.reshape(1, d))
    in_specs.append(pl.BlockSpec((1, d), lambda i: (0, 0)))
    outs, out_specs = [], []
    if out_x:
        outs.append(jax.ShapeDtypeStruct((t, d), F32)); out_specs.append(row)
    if out_h:
        outs.append(jax.ShapeDtypeStruct((t, d), BF16)); out_specs.append(row)
    if out_ht:
        outs.append(jax.ShapeDtypeStruct((d, t), BF16)); out_specs.append(col)
    if out_y:
        outs.append(jax.ShapeDtypeStruct((t, d), F32)); out_specs.append(row)
    body = functools.partial(_norm_body, has_ot=ot is not None, out_x=out_x, out_h=out_h,
                             out_ht=out_ht, out_y=out_y)
    return pl.pallas_call(body, grid=(t // tr,), in_specs=in_specs, out_specs=out_specs,
                          out_shape=outs, compiler_params=_params("parallel"), name="rms_norm")(*ins)


def _mm_body(x_ref, w_ref, *refs, epilogue):
    acc = jnp.dot(x_ref[...], w_ref[...], preferred_element_type=F32)
    epilogue(acc, *refs)


def _ep_store(acc, o_ref):
    o_ref[...] = acc.astype(o_ref.dtype)


def _ep_residual(acc, r_ref, o_ref):
    o_ref[...] = r_ref[...] + acc


def _ep_gate_first(acc, g_ref, o_ref):
    o_ref[...] = (_sigmoid(g_ref[...]) * acc).astype(o_ref.dtype)


def _ep_gate_add(acc, g_ref, p_ref, o_ref):
    o_ref[...] = (p_ref[...] + _sigmoid(g_ref[...]) * acc).astype(o_ref.dtype)


def _ep_small(acc, add_ref, alog_ref, o_ref, *, hg):
    z = acc + add_ref[...]
    o_ref[0:hg, :] = -jnp.exp(alog_ref[0:hg, :]) * _softplus(z[0:hg])
    o_ref[hg:2 * hg, :] = _sigmoid(z[hg:2 * hg])
    o_ref[2 * hg:3 * hg, :] = -_softplus(-z[2 * hg:3 * hg])


def _matmul(x, w, *, tm, tn, epilogue, extras=(), out_dtype=F32, name):
    m, k = x.shape
    n = w.shape[1]
    tm, tn = min(tm, m), min(tn, n)
    in_specs = [pl.BlockSpec((tm, k), lambda i, j: (i, 0)), pl.BlockSpec((k, tn), lambda i, j: (0, j))]
    ins = [x, w]
    for arr, kind, off in extras:
        ins.append(arr)
        if kind == "tile":
            in_specs.append(pl.BlockSpec((tm, tn), lambda i, j, off=off: (i, j + off)))
        else:
            in_specs.append(pl.BlockSpec(arr.shape, lambda i, j: (0, 0)))
    return pl.pallas_call(
        functools.partial(_mm_body, epilogue=epilogue), grid=(m // tm, n // tn), in_specs=in_specs,
        out_specs=pl.BlockSpec((tm, tn), lambda i, j: (i, j)),
        out_shape=jax.ShapeDtypeStruct((m, n), out_dtype),
        compiler_params=_params("parallel", "parallel"), name=name)(*ins)


def _gdn_pre_body(x_ref, buf_ref, cw_ref, o_ref, nb_ref, scr, *, seq_len, real_len, q_scale):
    seg = pl.program_id(1)
    kw = cw_ref.shape[0]
    x = x_ref[...]
    scr[0:8, :] = jnp.zeros((8, LANES), F32)
    scr[8 - (kw - 1):8, :] = buf_ref[...]
    scr[8:8 + seq_len, :] = x
    w = cw_ref[...]
    y = x * w[kw - 1:kw]
    for i in range(kw - 1):
        y = y + scr[8 - (kw - 1) + i:8 - (kw - 1) + i + seq_len, :] * w[i:i + 1]
    y = y * _sigmoid(y)
    n = lax.rsqrt(jnp.sum(y * y, axis=-1, keepdims=True) + L2_EPS)
    scale = jnp.where(seg == 0, n * q_scale, jnp.where(seg == 1, n, 1.0))
    rows = _iota((seq_len, 1), 0)
    o_ref[...] = jnp.where(rows < real_len, y * scale, 0.0)
    nb_ref[...] = scr[8 - (kw - 1) + real_len:8 + real_len, :]


def _gdn_pre(proj, col_blk, buf, conv_w, *, nseq, seq_len, real_len, row_blk, heads):
    kw = conv_w.shape[0]
    width = heads * LANES
    body = functools.partial(_gdn_pre_body, seq_len=seq_len, real_len=real_len, q_scale=LANES ** -0.5)
    return pl.pallas_call(
        body, grid=(nseq, 3, heads),
        in_specs=[pl.BlockSpec((seq_len, LANES), lambda s, g, h: (row_blk + s, col_blk + g * heads + h)),
                  pl.BlockSpec((None, kw - 1, LANES), lambda s, g, h: (s, 0, g * heads + h)),
                  pl.BlockSpec((kw, LANES), lambda s, g, h: (0, g * heads + h))],
        out_specs=[pl.BlockSpec((seq_len, LANES), lambda s, g, h: (s, g * heads + h)),
                   pl.BlockSpec((None, kw - 1, LANES), lambda s, g, h: (s, 0, g * heads + h))],
        out_shape=[jax.ShapeDtypeStruct((nseq * seq_len, 3 * width), F32),
                   jax.ShapeDtypeStruct((nseq, kw - 1, 3 * width), F32)],
        scratch_shapes=[pltpu.VMEM((seq_len + 8, LANES), F32)],
        compiler_params=_params("parallel", "parallel", "parallel"), name="gdn_conv")(proj, buf, conv_w)


def _chunk_masks():
    r, c = _iota((CHUNK, CHUNK), 0), _iota((CHUNK, CHUNK), 1)
    return r >= c, r > c


def _gdn_chunk_body(q_ref, k_ref, v_ref, g_ref, b_ref, o1_ref, q2_ref, m_ref, d_ref, *,
                    group, seq_len, real_len):
    rb, hg = pl.program_id(0), pl.program_id(1)
    lower, strict = _chunk_masks()
    ltri = lower.astype(F32)
    e0 = (_iota((CHUNK, CHUNK), 1) == 0).astype(F32)
    eye_d = _eye(LANES)
    steps = CHUNK.bit_length() - 1
    pos = (rb * ROWS + _iota((1, ROWS), 1)) % seq_len
    valid = pos < real_len
    for hh in range(group):
        h = hg * group + hh
        g_col = _col_bcast(jnp.where(valid, g_ref[pl.ds(h, 1), :], 0.0))
        b_col = _col_bcast(jnp.where(valid, b_ref[pl.ds(h, 1), :], 0.0))
        lanes = slice(hh * LANES, (hh + 1) * LANES)
        for c in range(ROWS // CHUNK):
            rows = slice(c * CHUNK, (c + 1) * CHUNK)
            q, k, v = q_ref[rows, lanes], k_ref[rows, lanes], v_ref[rows, lanes]
            gcum_c = _dot_hi(ltri, g_col[rows, 0:CHUNK])
            gcum_r = _dot_nt_hi(e0, gcum_c)
            decay = jnp.where(lower, jnp.exp(jnp.where(lower, gcum_c - gcum_r, 0.0)), 0.0)
            beta = b_col[rows, 0:1]
            gc = gcum_c[:, 0:1]
            kb = k * beta
            a_mat = jnp.where(strict, _dot_nt_hi(kb, k) * decay, 0.0)
            t_inv = _unit_lower_inverse(-a_mat, steps)
            eg = jnp.exp(gc)
            sol = _dot_hi(t_inv, jnp.concatenate([v * beta, kb * eg], axis=1))
            u, w = sol[:, 0:LANES], sol[:, LANES:2 * LANES]
            intra = jnp.where(lower, _dot_nt_hi(q, k) * decay, 0.0)
            g_last = gc[CHUNK - 1:CHUNK, :]
            kd_t = _transpose_hi(k * jnp.exp(g_last - gc))
            o1_ref[rows, lanes] = _dot_hi(intra, u)
            q2_ref[rows, lanes] = q * eg - _dot_hi(intra, w)
            m_ref[c, hh] = jnp.exp(g_last) * eye_d - _dot_hi(kd_t, w)
            d_ref[c, hh] = _dot_hi(kd_t, u)


def _gdn_chunk(qkv, small, *, lane_blk, heads, group, seq_len, real_len):
    rows_total = qkv.shape[0]
    hgs = heads // group
    gw = group * LANES
    hpad = small.shape[0] // 3
    nchunks = rows_total // CHUNK
    body = functools.partial(_gdn_chunk_body, group=group, seq_len=seq_len, real_len=real_len)
    tok = lambda off: pl.BlockSpec((ROWS, gw), lambda r, g, off=off: (r, off * hgs + g))
    mat = pl.BlockSpec((ROWS // CHUNK, group, LANES, LANES), lambda r, g: (r, g, 0, 0))
    return pl.pallas_call(
        body, grid=(rows_total // ROWS, hgs),
        in_specs=[tok(0), tok(1), tok(2),
                  pl.BlockSpec((hpad, ROWS), lambda r, g: (0, lane_blk + r)),
                  pl.BlockSpec((hpad, ROWS), lambda r, g: (1, lane_blk + r))],
        out_specs=[pl.BlockSpec((ROWS, gw), lambda r, g: (r, g)), pl.BlockSpec((ROWS, gw), lambda r, g: (r, g)),
                   mat, mat],
        out_shape=[jax.ShapeDtypeStruct((rows_total, heads * LANES), F32)] * 2
        + [jax.ShapeDtypeStruct((nchunks, heads, LANES, LANES), F32)] * 2,
        compiler_params=_params("parallel", "parallel"), name="gdn_chunk")(qkv, qkv, qkv, small, small)


def _scan_body(o1_ref, q2_ref, m_ref, d_ref, s0_ref, o_ref, sf_ref, s_scr, *, group, nchunk):
    c = pl.program_id(2)

    @pl.when(c == 0)
    def _():
        s_scr[...] = s0_ref[0]

    for hh in range(group):
        lanes = slice(hh * LANES, (hh + 1) * LANES)
        s = s_scr[hh]
        o_ref[:, lanes] = o1_ref[:, lanes] + _dot_hi(q2_ref[:, lanes], s)
        s_scr[hh] = _dot_hi(m_ref[0, hh], s) + d_ref[0, hh]

    @pl.when(c == nchunk - 1)
    def _():
        sf_ref[0] = s_scr[...]


def _scan(o1, q2, m, d, s0, *, nseq, heads, group):
    rows_total = o1.shape[0]
    nchunk = rows_total // CHUNK // nseq
    gw = group * LANES
    tok = pl.BlockSpec((CHUNK, gw), lambda s, g, c: (s * nchunk + c, g))
    mat = pl.BlockSpec((1, group, LANES, LANES), lambda s, g, c: (s * nchunk + c, g, 0, 0))
    st = pl.BlockSpec((1, group, LANES, LANES), lambda s, g, c: (s, g, 0, 0))
    return pl.pallas_call(
        functools.partial(_scan_body, group=group, nchunk=nchunk), grid=(nseq, heads // group, nchunk),
        in_specs=[tok, tok, mat, mat, st], out_specs=[tok, st],
        out_shape=[jax.ShapeDtypeStruct(o1.shape, F32), jax.ShapeDtypeStruct(s0.shape, F32)],
        scratch_shapes=[pltpu.VMEM((group, LANES, LANES), F32)],
        compiler_params=_params("parallel", "parallel", "arbitrary"), name="state_scan")(o1, q2, m, d, s0)


def _gdn_out_body(o_ref, z_ref, w_ref, y_ref, *, heads):
    for h in range(heads):
        lanes = slice(h * LANES, (h + 1) * LANES)
        o = o_ref[:, lanes]
        z = z_ref[:, lanes]
        o = o * lax.rsqrt(jnp.mean(o * o, axis=-1, keepdims=True) + NORM_EPS) * w_ref[...]
        y_ref[:, lanes] = (o * (z * _sigmoid(z))).astype(y_ref.dtype)


def _gdn_out(o, proj, norm_w, *, z_col_blk, row_blk, heads, tr=256):
    rows_total, width = o.shape
    tr = min(tr, rows_total)
    return pl.pallas_call(
        functools.partial(_gdn_out_body, heads=heads), grid=(rows_total // tr,),
        in_specs=[pl.BlockSpec((tr, width), lambda i: (i, 0)),
                  pl.BlockSpec((tr, width), lambda i: (row_blk * 1 + i, z_col_blk)),
                  pl.BlockSpec((1, LANES), lambda i: (0, 0))],
        out_specs=pl.BlockSpec((tr, width), lambda i: (i, 0)),
        out_shape=jax.ShapeDtypeStruct((rows_total, width), BF16),
        compiler_params=_params("parallel"), name="gdn_out")(o, proj, norm_w.reshape(1, LANES))


def _shift_body(x_ref, prev_ref, mu_ref, o_ref, last_ref, scr, *, seq_len, real_len):
    x = x_ref[...]
    w = x.shape[1]
    scr[0:8, :] = jnp.zeros((8, w), F32)
    scr[7:8, :] = prev_ref[...]
    scr[8:8 + seq_len, :] = x
    o_ref[...] = x + mu_ref[...] * (scr[7:7 + seq_len, :] - x)
    last_ref[...] = x[real_len - 1:real_len, :]


def _shift(proj, col_blk, prev, mu, *, nseq, seq_len, real_len, row_blk, tw):
    width = mu.shape[-1]
    tw = min(tw, width)
    body = functools.partial(_shift_body, seq_len=seq_len, real_len=real_len)
    return pl.pallas_call(
        body, grid=(nseq, width // tw),
        in_specs=[pl.BlockSpec((seq_len, tw), lambda s, j: (row_blk + s, col_blk + j)),
                  pl.BlockSpec((None, 1, tw), lambda s, j: (s, 0, j)),
                  pl.BlockSpec((1, tw), lambda s, j: (0, j))],
        out_specs=[pl.BlockSpec((seq_len, tw), lambda s, j: (s, j)),
                   pl.BlockSpec((None, 1, tw), lambda s, j: (s, 0, j))],
        out_shape=[jax.ShapeDtypeStruct((nseq * seq_len, width), F32),
                   jax.ShapeDtypeStruct((nseq, 1, width), F32)],
        scratch_shapes=[pltpu.VMEM((seq_len + 8, tw), F32)],
        compiler_params=_params("parallel", "parallel"), name="rwkv_shift")(
            proj, prev.reshape(nseq, 1, width), mu.reshape(1, width))


def _rwkv_chunk_body(r_ref, k_ref, v_ref, lo_ref, w2_ref, a2_ref, g2_ref, pv_ref,
                     o1_ref, q2_ref, m_ref, d_ref, bonus_ref, gate_ref, *, seq_len, real_len, lw, la):
    rb = pl.program_id(0)
    lower, strict = _chunk_masks()
    ltri = lower.astype(F32)
    eye_c = _eye(CHUNK)
    eye_d = _eye(LANES)
    steps = CHUNK.bit_length() - 1
    lane = _iota((1, LANES), 1)
    head_masks = [(lane // RWKV_HEAD == i).astype(F32) for i in range(LANES // RWKV_HEAD)]
    same_head = (_iota((LANES, LANES), 0) // RWKV_HEAD == _iota((LANES, LANES), 1) // RWKV_HEAD).astype(F32)
    valid = ((rb * ROWS + _iota((ROWS, 1), 0)) % seq_len) < real_len
    w0, a0, k_k, k_a, r_k = (pv_ref[i:i + 1, :] for i in range(5))

    lo = lo_ref[...]
    log_w = RWKV_DECAY_SCALE * _sigmoid(w0 + _dot(jnp.tanh(lo[:, 0:lw]), w2_ref[...]))
    iclr = _sigmoid(a0 + _dot(lo[:, lw:lw + la], a2_ref[...]))
    gate_ref[...] = _dot(_sigmoid(lo[:, lw + la:]), g2_ref[...])
    r_all, k_raw, v_all = r_ref[...], k_ref[...], v_ref[...]
    kk = k_raw * k_k
    kk = kk * lax.rsqrt(_dot_hi(kk * kk, same_head) + L2_EPS)
    k_all = k_raw * (1.0 + (iclr - 1.0) * k_a)
    bonus_ref[...] = _dot_hi(r_all * k_all * r_k, same_head) * v_all
    zero = jnp.zeros_like(r_all)
    r_all, k_all, v_all = (jnp.where(valid, t, zero) for t in (r_all, k_all, v_all))
    a_all = jnp.where(valid, -kk, zero)
    b_all = jnp.where(valid, kk * iclr, zero)
    log_w = jnp.where(valid, log_w, zero)

    for c in range(ROWS // CHUNK):
        rows = slice(c * CHUNK, (c + 1) * CHUNK)
        r, k, v, a, b, lg = (t[rows] for t in (r_all, k_all, v_all, a_all, b_all, log_w))
        cum = _dot_hi(ltri, lg)
        dec, dec_inv = jnp.exp(cum), jnp.exp(-cum)
        a_h = a * jnp.exp(cum - lg)
        b_t, k_t, r_h = b * dec_inv, k * dec_inv, r * dec
        dec_end = dec[CHUNK - 1:CHUNK, :]
        p1 = p2 = o1 = q2 = None
        for hm in head_masks:
            a_m, r_m = a_h * hm, r_h * hm
            n_ab = jnp.where(strict, _dot_nt_hi(a_m, b_t), 0.0)
            n_ak = jnp.where(strict, _dot_nt_hi(a_m, k_t), 0.0)
            t_inv = _unit_lower_inverse(n_ab, steps)
            rb_m = jnp.where(lower, _dot_nt_hi(r_m, b_t), 0.0)
            rk_m = jnp.where(lower, _dot_nt_hi(r_m, k_t), 0.0)
            p1_m = _dot_hi(t_inv, _dot_hi(n_ak, v))
            p2_m = _dot_hi(t_inv, a_h)
            o1_m = _dot_hi(rb_m, p1_m) + _dot_hi(rk_m, v)
            q2_m = _dot_hi(rb_m, p2_m)
            if p1 is None:
                p1, p2, o1, q2 = p1_m * hm, p2_m * hm, o1_m * hm, q2_m * hm
            else:
                p1, p2, o1, q2 = p1 + p1_m * hm, p2 + p2_m * hm, o1 + o1_m * hm, q2 + q2_m * hm
        bg_t = _transpose_hi(b_t * dec_end)
        kg_t = _transpose_hi(k_t * dec_end)
        o1_ref[rows, :] = o1
        q2_ref[rows, :] = r_h + q2
        m_ref[c, 0] = eye_d * dec_end + same_head * _dot_hi(bg_t, p2)
        d_ref[c, 0] = same_head * (_dot_hi(bg_t, p1) + _dot_hi(kg_t, v))
    del eye_c


def _rwkv_chunk(xs, xs_lora, w2, a2, g2, pvec, *, blocks, seq_len, real_len):
    rows_total = xs.shape[0]
    nchunks = rows_total // CHUNK
    lw, la, lgd = w2.shape[0], a2.shape[0], g2.shape[0]
    body = functools.partial(_rwkv_chunk_body, seq_len=seq_len, real_len=real_len, lw=lw, la=la)
    tok = lambda off: pl.BlockSpec((ROWS, LANES), lambda r, j, off=off: (r, off * blocks + j))
    out_tok = pl.BlockSpec((ROWS, LANES), lambda r, j: (r, j))
    mat = pl.BlockSpec((ROWS // CHUNK, 1, LANES, LANES), lambda r, j: (r, j, 0, 0))
    width = blocks * LANES
    return pl.pallas_call(
        body, grid=(rows_total // ROWS, blocks),
        in_specs=[tok(0), tok(1), tok(2),
                  pl.BlockSpec((ROWS, lw + la + lgd), lambda r, j: (r, 0)),
                  pl.BlockSpec((lw, LANES), lambda r, j: (0, j)),
                  pl.BlockSpec((la, LANES), lambda r, j: (0, j)),
                  pl.BlockSpec((lgd, LANES), lambda r, j: (0, j)),
                  pl.BlockSpec((8, LANES), lambda r, j: (0, j))],
        out_specs=[out_tok, out_tok, mat, mat, out_tok, out_tok],
        out_shape=[jax.ShapeDtypeStruct((rows_total, width), F32)] * 2
        + [jax.ShapeDtypeStruct((nchunks, blocks, LANES, LANES), F32)] * 2
        + [jax.ShapeDtypeStruct((rows_total, width), F32)] * 2,
        compiler_params=_params("parallel", "parallel"), name="rwkv_chunk")(
            xs, xs, xs, xs_lora, w2, a2, g2, pvec)


def _rwkv_out_body(o_ref, bonus_ref, gate_ref, lnw_ref, lnb_ref, y_ref):
    same_head = (_iota((LANES, LANES), 0) // RWKV_HEAD
                 == _iota((LANES, LANES), 1) // RWKV_HEAD).astype(F32) * (1.0 / RWKV_HEAD)
    o = o_ref[...]
    mu = _dot_hi(o, same_head)
    var = _dot_hi((o - mu) * (o - mu), same_head)
    y = (o - mu) * lax.rsqrt(var + RWKV_GN_EPS) * lnw_ref[...] + lnb_ref[...]
    y_ref[...] = ((y + bonus_ref[...]) * gate_ref[...]).astype(y_ref.dtype)


def _rwkv_out(o, bonus, gate, ln_w, ln_b, tr=256):
    rows_total, width = o.shape
    tr = min(tr, rows_total)
    tok = pl.BlockSpec((tr, LANES), lambda i, j: (i, j))
    vec = pl.BlockSpec((1, LANES), lambda i, j: (0, j))
    return pl.pallas_call(
        _rwkv_out_body, grid=(rows_total // tr, width // LANES),
        in_specs=[tok, tok, tok, vec, vec], out_specs=tok,
        out_shape=jax.ShapeDtypeStruct((rows_total, width), BF16),
        compiler_params=_params("parallel", "parallel"), name="rwkv_out")(
            o, bonus, gate, ln_w.reshape(1, width), ln_b.reshape(1, width))


def _fox_cumsum_body(lf_ref, o_ref, carry):
    @pl.when(pl.program_id(1) == 0)
    def _():
        carry[...] = jnp.zeros_like(carry)

    upper = (_iota((LANES, LANES), 0) <= _iota((LANES, LANES), 1)).astype(F32)
    f = _dot_hi(lf_ref[...], upper) + carry[...]
    o_ref[...] = f
    carry[...] = jnp.broadcast_to(f[:, LANES - 1:LANES], f.shape)


def _fox_cumsum(small, *, nseq, seq_len):
    hpad = small.shape[0] // 3
    nb = seq_len // LANES
    return pl.pallas_call(
        _fox_cumsum_body, grid=(nseq, nb),
        in_specs=[pl.BlockSpec((hpad, LANES), lambda s, j: (2, s * nb + j))],
        out_specs=pl.BlockSpec((hpad, LANES), lambda s, j: (0, s * nb + j)),
        out_shape=jax.ShapeDtypeStruct((hpad, nseq * seq_len), F32),
        scratch_shapes=[pltpu.VMEM((hpad, LANES), F32)],
        compiler_params=_params("parallel", "arbitrary"), name="fox_cumsum")(small)


def _fox_prompt_body(q_ref, k_ref, v_ref, fk_ref, fq_ref, o_ref, *, scale):
    h, qi = pl.program_id(1), pl.program_id(2)
    tq, seq_len = q_ref.shape[0], k_ref.shape[0]
    s = _dot_nt(q_ref[...], k_ref[...]) * scale
    f_k = fk_ref[pl.ds(h, 1), :]
    f_q = _col_bcast(fq_ref[pl.ds(h, 1), :])[:, 0:1]
    logit = s + f_q - f_k
    row = qi * tq + _iota((tq, seq_len), 0)
    logit = jnp.where(_iota((tq, seq_len), 1) <= row, logit, NEG)
    p = jnp.exp(logit - jnp.max(logit, axis=-1, keepdims=True))
    o = _dot(p, v_ref[...]) / jnp.sum(p, axis=-1, keepdims=True)
    o_ref[...] = o.astype(o_ref.dtype)


def _fox_prompt(proj, f_cum, *, col_blk, nseq, seq_len, heads):
    nq = seq_len // LANES
    hpad = f_cum.shape[0]
    body = functools.partial(_fox_prompt_body, scale=LANES ** -0.5)
    return pl.pallas_call(
        body, grid=(nseq, heads, nq),
        in_specs=[pl.BlockSpec((LANES, LANES), lambda b, h, i: (b * nq + i, col_blk + h)),
                  pl.BlockSpec((seq_len, LANES), lambda b, h, i: (b, col_blk + heads + h)),
                  pl.BlockSpec((seq_len, LANES), lambda b, h, i: (b, col_blk + 2 * heads + h)),
                  pl.BlockSpec((hpad, seq_len), lambda b, h, i: (0, b)),
                  pl.BlockSpec((hpad, LANES), lambda b, h, i: (0, b * nq + i))],
        out_specs=pl.BlockSpec((LANES, LANES), lambda b, h, i: (b * nq + i, h)),
        out_shape=jax.ShapeDtypeStruct((nseq * seq_len, heads * LANES), BF16),
        compiler_params=_params("parallel", "parallel", "parallel"), name="fox_prompt")(
            proj, proj, proj, f_cum, f_cum)


def _fox_sample_body(pt_ref, q_ref, kn_ref, vn_ref, fn_ref, fb_ref, kp_ref, vp_ref, lp_ref, o_ref,
                     qbd, m_s, l_s, acc, carry, cn_s, *, heads, real_len, scale):
    p_idx = pl.program_id(1)
    hq = heads * real_len
    width = heads * LANES
    col = _iota((1, hq), 1)
    expand = (_iota((heads, hq), 1) // real_len == _iota((heads, hq), 0)).astype(F32)
    eye_hq = _eye(hq)

    @pl.when(p_idx == 0)
    def _():
        sel_t = (_iota((hq, 8), 0) % real_len == _iota((hq, 8), 1)).astype(F32)
        q_full = _dot_hi(sel_t, q_ref[0:8, :])
        own = _iota((hq, width), 1) // LANES == _iota((hq, width), 0) // real_len
        qbd[...] = jnp.where(own, q_full, 0.0).astype(BF16)
        rows8 = _iota((8, hq), 0)
        lf = jnp.where(_iota((8, heads), 0) < real_len, -_softplus(-(fn_ref[0:8, 0:heads] + fb_ref[...])), 0.0)
        tri8 = (_iota((8, 8), 0) >= _iota((8, 8), 1)).astype(F32)
        cn_e = _dot_hi(_dot_hi(tri8, lf), expand)
        cn_row = jnp.sum(jnp.where(rows8 == col % real_len, cn_e, 0.0), axis=0, keepdims=True)
        cn_s[...] = cn_row
        logit = _dot_nt(kn_ref[0:8, :], qbd[...]) * scale + cn_row - cn_e
        logit = jnp.where(rows8 <= col % real_len, logit, NEG)
        m0 = jnp.max(logit, axis=0, keepdims=True)
        p = jnp.exp(logit - m0)
        m_s[...] = m0
        l_s[...] = jnp.sum(p, axis=0, keepdims=True)
        acc[...] = _dot(_transpose_hi(p), vn_ref[0:8, :])
        carry[...] = jnp.zeros_like(carry)

    lfp = lp_ref[...]
    page = lfp.shape[0]
    later = (_iota((page, page), 1) > _iota((page, page), 0)).astype(F32)
    lfp_e = _dot_hi(lfp, expand)
    suffix = _dot_hi(later, lfp_e) + carry[...]
    logit = _dot_nt(kp_ref[...], qbd[...]) * scale + cn_s[...] + suffix
    m_old = m_s[...]
    m_new = jnp.maximum(m_old, jnp.max(logit, axis=0, keepdims=True))
    alpha = jnp.exp(m_old - m_new)
    p = jnp.exp(logit - m_new)
    l_s[...] = alpha * l_s[...] + jnp.sum(p, axis=0, keepdims=True)
    m_s[...] = m_new
    acc[...] = acc[...] * _col_bcast(alpha)[:, 0:1] + _dot(_dot_nt(eye_hq, p), vp_ref[...])
    carry[...] = carry[...] + jnp.sum(lfp_e, axis=0, keepdims=True)

    @pl.when(p_idx == pl.num_programs(1) - 1)
    def _():
        out = acc[...] / _col_bcast(l_s[...])[:, 0:1]
        o_ref[...] = jnp.zeros_like(o_ref)
        for h in range(heads):
            o_ref[0:real_len, h * LANES:(h + 1) * LANES] = out[
                h * real_len:(h + 1) * real_len, h * LANES:(h + 1) * LANES].astype(o_ref.dtype)


def _fox_sample(page_table, proj, lora, f_bias, cache_k, cache_v, cache_lf, *, layer, col_blk, lora_blk,
                nseq, seq_len, real_len, row_blk, heads):
    n_pages = page_table.shape[1]
    width = heads * LANES
    page = cache_k.shape[2]
    hq = heads * real_len
    body = functools.partial(_fox_sample_body, heads=heads, real_len=real_len, scale=LANES ** -0.5)
    new = lambda off: pl.BlockSpec((seq_len, width), lambda s, p, pt, off=off: (row_blk + s, col_blk + off))
    pick = lambda s, p, pt: pt[s * n_pages + n_pages - 1 - p]
    grid_spec = pltpu.PrefetchScalarGridSpec(
        num_scalar_prefetch=1, grid=(nseq, n_pages),
        in_specs=[new(0), new(1), new(2),
                  pl.BlockSpec((seq_len, LANES), lambda s, p, pt: (row_blk + s, lora_blk)),
                  pl.BlockSpec((1, heads), lambda s, p, pt: (0, 0)),
                  pl.BlockSpec((None, None, page, width), lambda s, p, pt: (layer, pick(s, p, pt), 0, 0)),
                  pl.BlockSpec((None, None, page, width), lambda s, p, pt: (layer, pick(s, p, pt), 0, 0)),
                  pl.BlockSpec((None, None, page, heads), lambda s, p, pt: (layer, pick(s, p, pt), 0, 0))],
        out_specs=pl.BlockSpec((seq_len, width), lambda s, p, pt: (s, 0)),
        scratch_shapes=[pltpu.VMEM((hq, width), BF16), pltpu.VMEM((1, hq), F32), pltpu.VMEM((1, hq), F32),
                        pltpu.VMEM((hq, width), F32), pltpu.VMEM((1, hq), F32), pltpu.VMEM((1, hq), F32)])
    return pl.pallas_call(
        body, grid_spec=grid_spec, out_shape=jax.ShapeDtypeStruct((nseq * seq_len, width), F32),
        compiler_params=_params("parallel", "arbitrary"), name="fox_sample")(
            page_table.reshape(-1), proj, proj, proj, lora, f_bias.reshape(1, heads), cache_k, cache_v, cache_lf)


def _top_values(s, count):
    n = s.shape[0]
    idx = _iota(s.shape, 0).astype(F32)
    vals = []
    for _ in range(count):
        m = jnp.max(s, axis=0, keepdims=True)
        first = jnp.min(jnp.where(s == m, idx, float(n)), axis=0, keepdims=True)
        s = jnp.where(idx == first, NEG, s)
        vals.append(m)
    return vals


def _peer_topk_body(q_ref, sk_ref, s1_ref, s2_ref, tau_ref, off_ref, *, heads, half):
    n_keys = sk_ref.shape[2]
    for h in range(heads):
        base = h * 2 * half
        s1 = _dot_hi(sk_ref[h, 0], q_ref[base:base + half, :])
        s2 = _dot_hi(sk_ref[h, 1], q_ref[base + half:base + 2 * half, :])
        s1_ref[h * n_keys:(h + 1) * n_keys, :] = s1
        s2_ref[h * n_keys:(h + 1) * n_keys, :] = s2
        v1 = _top_values(s1, PEER_TOPK)
        v2 = jnp.concatenate(_top_values(s2, PEER_TOPK), axis=0)
        cand = jnp.concatenate([a + v2 for a in v1], axis=0)
        top = jnp.concatenate(_top_values(cand, PEER_TOPK), axis=0)
        smax = top[0:1]
        tau_ref[h:h + 1, :] = top[PEER_TOPK - 1:PEER_TOPK]
        off_ref[h:h + 1, :] = smax + jnp.log(jnp.sum(jnp.exp(top - smax), axis=0, keepdims=True))


def _peer_topk(q_t, subkeys, tt=256):
    heads, _, n_keys, half = subkeys.shape
    t = q_t.shape[1]
    body = functools.partial(_peer_topk_body, heads=heads, half=half)
    score = pl.BlockSpec((heads * n_keys, tt), lambda i: (0, i))
    stat = pl.BlockSpec((heads, tt), lambda i: (0, i))
    return pl.pallas_call(
        body, grid=(t // tt,),
        in_specs=[pl.BlockSpec((q_t.shape[0], tt), lambda i: (0, i)),
                  pl.BlockSpec(subkeys.shape, lambda i: (0, 0, 0, 0))],
        out_specs=[score, score, stat, stat],
        out_shape=[jax.ShapeDtypeStruct((heads * n_keys, t), F32)] * 2
        + [jax.ShapeDtypeStruct((heads, t), F32)] * 2,
        compiler_params=_params("parallel"), name="peer_topk")(q_t, subkeys)


def _peer_dense_body(ht_ref, u_ref, vt_ref, s1_ref, s2_ref, tau_ref, off_ref, o_ref, *, heads, n_keys):
    j = pl.program_id(1)

    @pl.when(j == 0)
    def _():
        o_ref[...] = jnp.zeros_like(o_ref)

    hb = jnp.dot(u_ref[...], ht_ref[...], preferred_element_type=F32)
    act = 0.5 * hb * (1.0 + lax.erf(hb * (2.0 ** -0.5)))
    per_step = u_ref.shape[0] // n_keys
    gates = []
    for e in range(per_step):
        e1 = j * per_step + e
        g = None
        for h in range(heads):
            s = s1_ref[pl.ds(h * n_keys + e1, 1), :] + s2_ref[h * n_keys:(h + 1) * n_keys, :]
            term = jnp.where(s >= tau_ref[h:h + 1, :], jnp.exp(s - off_ref[h:h + 1, :]), 0.0)
            g = term if g is None else g + term
        gates.append(g)
    a = (jnp.concatenate(gates, axis=0) * act).astype(BF16)
    o_ref[...] += jnp.dot(vt_ref[...], a, preferred_element_type=F32)


def _peer_dense(h_t, u, v_t, s1, s2, tau, off, *, heads, n_keys, tm=512, te=256):
    d, t = h_t.shape
    n_exp = u.shape[0]
    body = functools.partial(_peer_dense_body, heads=heads, n_keys=n_keys)
    score = pl.BlockSpec((heads * n_keys, tm), lambda i, j: (0, i))
    stat = pl.BlockSpec((heads, tm), lambda i, j: (0, i))
    return pl.pallas_call(
        body, grid=(t // tm, n_exp // te),
        in_specs=[pl.BlockSpec((d, tm), lambda i, j: (0, i)),
                  pl.BlockSpec((te, d), lambda i, j: (j, 0)),
                  pl.BlockSpec((d, te), lambda i, j: (0, j)),
                  score, score, stat, stat],
        out_specs=pl.BlockSpec((d, tm), lambda i, j: (0, i)),
        out_shape=jax.ShapeDtypeStruct((d, t), F32),
        compiler_params=_params("parallel", "arbitrary"), name="peer_dense")(h_t, u, v_t, s1, s2, tau, off)


def _round_up(x, m):
    return (x + m - 1) // m * m


def kernel(x_prompt, x_sample, cache_fox_k, cache_fox_v, cache_fox_logf, state_gdn_conv, state_gdn_S, state_rwkv_shift, state_rwkv_S, page_table, meta_tokens, norm_mix, norm_ffn, norm_final, w_in, gdn_conv_w, gdn_A_log, gdn_dt_bias, gdn_norm_w, fox_f_bias, rwkv_mu, rwkv_w0, rwkv_w2, rwkv_a0, rwkv_a2, rwkv_g2, rwkv_k_k, rwkv_k_a, rwkv_r_k, rwkv_ln_w, rwkv_ln_b, w_branch_gdn, w_branch_fox, w_branch_rwkv, w_out, peer_wq, peer_subkeys, peer_u, peer_v):
    n_p, seq, d = x_prompt.shape
    n_s, dec_seq, _ = x_sample.shape
    depth = w_in.shape[0]
    n_meta = meta_tokens.shape[0]
    h_g = state_gdn_S.shape[2]
    h_f = cache_fox_k.shape[3]
    h_r = state_rwkv_S.shape[2]
    w_g, w_f, w_r = h_g * LANES, h_f * LANES, h_r * RWKV_HEAD
    assert state_gdn_S.shape[3] == LANES and cache_fox_k.shape[4] == LANES and state_rwkv_S.shape[3] == RWKV_HEAD
    assert w_g == w_f == w_r and h_g == h_f
    lw, la, lg = rwkv_w2.shape[1], rwkv_a2.shape[1], rwkv_g2.shape[1]
    lora_w = lw + la + lg
    assert lora_w % LANES == 0 and h_f <= LANES
    conv_k = gdn_conv_w.shape[1]
    p_heads, _, n_keys, p_half = peer_subkeys.shape[1:]

    lp_real, ls_real = n_meta + seq, dec_seq
    lp = _round_up(lp_real, LANES)
    ls = _round_up(max(ls_real, CHUNK), CHUNK)
    t_p, t_s = n_p * lp, n_s * ls
    t_pad = t_p + t_s
    assert t_pad % 512 == 0 and t_p % ROWS == 0 and t_s % ROWS == 0 and ls_real <= 8
    hpad = _round_up(h_g, 8)

    c_gqkv, c_gz, c_fq, c_rr, c_gate = 0, 3 * w_g, 4 * w_g, 4 * w_g + 3 * w_f, 4 * w_g + 3 * w_f + 3 * w_r
    n_main = c_gate + 3 * d
    o_fox = 4 * w_g + 2 * h_g
    o_rwkv = o_fox + 3 * w_f + h_f
    o_gate = o_rwkv + 3 * w_r + lora_w

    xp = jnp.concatenate([jnp.broadcast_to(meta_tokens[None], (n_p, n_meta, d)), x_prompt], axis=1)
    xp = jnp.pad(xp, ((0, 0), (0, lp - lp_real), (0, 0))).reshape(t_p, d)
    xs = jnp.pad(x_sample, ((0, 0), (0, ls - ls_real), (0, 0))).reshape(t_s, d)
    x = jnp.concatenate([xp, xs], axis=0)

    tm = t_pad // 8 if (t_pad // 8) % 16 == 0 else 512
    p_rows = lambda a: a[:t_p].reshape(n_p, lp, -1)[:, :lp_real]
    s_rows = lambda a: a[t_p:].reshape(n_s, ls, -1)[:, :ls_real]
    zeros = lambda *s: jnp.zeros(s, F32)
    pad_rows = lambda a, n: jnp.pad(a, ((0, n - a.shape[0]), (0, 0)))

    outs_p, outs_s = [], []
    o_t = None
    for l in range(depth):
        wl = w_in[l]
        w_main = jnp.concatenate([wl[:, 0:4 * w_g], wl[:, o_fox:o_fox + 3 * w_f], wl[:, o_rwkv:o_rwkv + 3 * w_r],
                                  wl[:, o_gate:o_gate + 3 * d]], axis=1).astype(BF16)
        w_lora = jnp.concatenate([wl[:, o_rwkv + 3 * w_r:o_rwkv + 3 * w_r + lora_w],
                                  wl[:, o_fox + 3 * w_f:o_fox + 3 * w_f + h_f],
                                  jnp.zeros((d, LANES - h_f), F32)], axis=1).astype(BF16)
        w_small_t = jnp.concatenate([pad_rows(wl[:, 4 * w_g:4 * w_g + h_g].T, hpad),
                                     pad_rows(wl[:, 4 * w_g + h_g:4 * w_g + 2 * h_g].T, hpad),
                                     pad_rows(wl[:, o_fox + 3 * w_f:o_fox + 3 * w_f + h_f].T, hpad)], axis=0).astype(BF16)
        col = lambda v: pad_rows(v.reshape(-1, 1), hpad)
        small_add = jnp.concatenate([col(gdn_dt_bias[l]), zeros(hpad, 1), col(fox_f_bias[l])], axis=0)
        small_alog = jnp.concatenate([col(gdn_A_log[l]), zeros(2 * hpad, 1)], axis=0)

        if l == 0:
            h, h_t = _norm(x, norm_mix[l], out_h=True, out_ht=True)
        else:
            x, h, h_t = _norm(x, norm_mix[l], o_t, out_x=True, out_h=True, out_ht=True)

        proj = _matmul(h, w_main, tm=tm, tn=512, epilogue=_ep_store, name="in_proj")
        lora = _matmul(h, w_lora, tm=tm, tn=lora_w + LANES, epilogue=_ep_store, name="in_proj_lora")
        small = _matmul(w_small_t, h_t, tm=3 * hpad, tn=512, epilogue=functools.partial(_ep_small, hg=hpad),
                        extras=[(small_add, "vec", 0), (small_alog, "vec", 0)], name="in_proj_heads")

        groups = [dict(nseq=n_p, seq_len=lp, real_len=lp_real, row0=0),
                  dict(nseq=n_s, seq_len=ls, real_len=ls_real, row0=t_p)]
        states = [dict(conv=zeros(n_p, conv_k - 1, 3 * w_g), gdn_s=zeros(n_p, h_g, LANES, LANES),
                       shift=zeros(n_p, 3 * w_r + lora_w), rwkv_s=zeros(n_p, h_r, RWKV_HEAD, RWKV_HEAD)),
                  dict(conv=state_gdn_conv[l], gdn_s=state_gdn_S[l], shift=state_rwkv_shift[l],
                       rwkv_s=state_rwkv_S[l])]
        y_gdn, y_rwkv, res = [], [], []
        for grp, st in zip(groups, states):
            nseq, sl, rl, row0 = grp["nseq"], grp["seq_len"], grp["real_len"], grp["row0"]
            qkv, conv_new = _gdn_pre(proj, c_gqkv // LANES, st["conv"], gdn_conv_w[l], nseq=nseq, seq_len=sl,
                                     real_len=rl, row_blk=row0 // sl, heads=h_g)
            g_grp = min(4, h_g)
            o1, q2, m_mat, d_mat = _gdn_chunk(qkv, small, lane_blk=row0 // ROWS, heads=h_g, group=g_grp,
                                              seq_len=sl, real_len=rl)
            o_gdn, gdn_s = _scan(o1, q2, m_mat, d_mat, st["gdn_s"], nseq=nseq, heads=h_g, group=min(8, h_g))
            y_gdn.append(_gdn_out(o_gdn, proj, gdn_norm_w[l], z_col_blk=c_gz // w_g,
                                  row_blk=row0 // min(256, nseq * sl), heads=h_g))
            mu = rwkv_mu[l]
            xs_main, last_main = _shift(proj, c_rr // 256, st["shift"][:, :3 * w_r], mu[:3 * w_r], nseq=nseq,
                                        seq_len=sl, real_len=rl, row_blk=row0 // sl, tw=256)
            xs_lora, last_lora = _shift(lora, 0, st["shift"][:, 3 * w_r:], mu[3 * w_r:], nseq=nseq, seq_len=sl,
                                        real_len=rl, row_blk=row0 // sl, tw=lora_w)
            pvec = jnp.concatenate([rwkv_w0[l][None], rwkv_a0[l][None], rwkv_k_k[l][None], rwkv_k_a[l][None],
                                    rwkv_r_k[l].reshape(1, w_r), zeros(3, w_r)], axis=0)
            blocks = w_r // LANES
            ro1, rq2, rm, rd, bonus, gate = _rwkv_chunk(xs_main, xs_lora, rwkv_w2[l], rwkv_a2[l], rwkv_g2[l], pvec,
                                                        blocks=blocks, seq_len=sl, real_len=rl)
            hp = LANES // RWKV_HEAD
            s0 = jnp.swapaxes(st["rwkv_s"], -1, -2).reshape(nseq, blocks, hp, RWKV_HEAD, RWKV_HEAD)
            s0 = jnp.einsum("sbikv,ij->sbikjv", s0, jnp.eye(hp, dtype=F32)).reshape(nseq, blocks, LANES, LANES)
            ro, rwkv_sb = _scan(ro1, rq2, rm, rd, s0, nseq=nseq, heads=blocks, group=min(8, blocks))
            y_rwkv.append(_rwkv_out(ro, bonus, gate, rwkv_ln_w[l], rwkv_ln_b[l]))
            sb = rwkv_sb.reshape(nseq, blocks, hp, RWKV_HEAD, hp, RWKV_HEAD)
            rwkv_s = jnp.stack([sb[:, :, i, :, i, :] for i in range(hp)], axis=2)
            rwkv_s = jnp.swapaxes(rwkv_s, -1, -2).reshape(nseq, h_r, RWKV_HEAD, RWKV_HEAD)
            shift_new = jnp.concatenate([last_main[:, 0], last_lora[:, 0]], axis=-1)
            res.append((conv_new, gdn_s, shift_new, rwkv_s))

        f_cum = _fox_cumsum(small, nseq=n_p, seq_len=lp)
        y_fox_p = _fox_prompt(proj, f_cum, col_blk=c_fq // LANES, nseq=n_p, seq_len=lp, heads=h_f)
        y_fox_s = _fox_sample(page_table, proj, lora, fox_f_bias[l],
                              cache_fox_k.reshape(cache_fox_k.shape[:3] + (w_f,)),
                              cache_fox_v.reshape(cache_fox_v.shape[:3] + (w_f,)), cache_fox_logf,
                              layer=l, col_blk=c_fq // w_f, lora_blk=lora_w // LANES, nseq=n_s, seq_len=ls,
                              real_len=ls_real, row_blk=t_p // ls, heads=h_f)
        y_g = jnp.concatenate(y_gdn, axis=0)
        y_f = jnp.concatenate([y_fox_p, y_fox_s.astype(BF16)], axis=0)
        y_r = jnp.concatenate(y_rwkv, axis=0)

        gate_blk = c_gate // 512
        merged = _matmul(y_g, w_branch_gdn[l].astype(BF16), tm=tm, tn=512, epilogue=_ep_gate_first,
                         extras=[(proj, "tile", gate_blk)], name="branch_gdn")
        merged = _matmul(y_f, w_branch_fox[l].astype(BF16), tm=tm, tn=512, epilogue=_ep_gate_add,
                         extras=[(proj, "tile", gate_blk + d // 512), (merged, "tile", 0)], name="branch_fox")
        merged = _matmul(y_r, w_branch_rwkv[l].astype(BF16), tm=tm, tn=512, epilogue=_ep_gate_add,
                         extras=[(proj, "tile", gate_blk + 2 * d // 512), (merged, "tile", 0)], out_dtype=BF16,
                         name="branch_rwkv")
        x = _matmul(merged, w_out[l].astype(BF16), tm=tm, tn=512, epilogue=_ep_residual,
                    extras=[(x, "tile", 0)], name="out_proj")

        (hf_t,) = _norm(x, norm_ffn[l], out_ht=True)
        q_t = _matmul(peer_wq[l].T.astype(BF16), hf_t, tm=512, tn=512, epilogue=_ep_store, name="peer_query")
        s1, s2, tau, off = _peer_topk(q_t, peer_subkeys[l])
        o_t = _peer_dense(hf_t, peer_u[l].astype(BF16), peer_v[l].T.astype(BF16), s1, s2, tau, off,
                          heads=p_heads, n_keys=n_keys)

        fk = proj[:, c_fq + w_f:c_fq + 2 * w_f]
        fv = proj[:, c_fq + 2 * w_f:c_fq + 3 * w_f]
        lf = small[2 * hpad:2 * hpad + h_f].T
        shape_kv = lambda a: a.reshape(a.shape[0], a.shape[1], h_f, LANES)
        (cp, gsp, shp, rsp), (cs, gss, shs, rss) = res
        outs_p.append((shape_kv(p_rows(fk)), shape_kv(p_rows(fv)), p_rows(lf), cp, gsp, shp, rsp))
        outs_s.append((shape_kv(s_rows(fk)), shape_kv(s_rows(fv)), s_rows(lf), cs, gss, shs, rss))

    (y,) = _norm(x, norm_final, o_t, out_y=True)
    y_prompt = p_rows(y)[:, n_meta:]
    y_sample = s_rows(y)
    stack = lambda lst, i: jnp.stack([s[i] for s in lst])
    return ((y_prompt, y_sample) + tuple(stack(outs_p, i) for i in range(7))
            + tuple(stack(outs_s, i) for i in range(7)))
```
